```python
import jax, jax.numpy as jnp
from jax import lax
import numpy as np

D_MODEL = 1024
BATCH = 16
SEQ = 4096
DEPTH = 1

PLE_DIM = 256
N_HEADS = 16
N_KV_HEADS = 2
HEAD_DIM = 64
GROUP = N_HEADS // N_KV_HEADS
ATTN_WIDTH = N_HEADS * HEAD_DIM
KV_WIDTH = N_KV_HEADS * HEAD_DIM
CONV_WIDTH = D_MODEL
CONV_KERNEL = 31
WINDOW = 128
BLOCK = 128
ROPE_DIM = HEAD_DIM // 4
ROPE_THETA = 500000.0
N_BRANCH = 2
EPS = 1e-6
MAX_POS_OFFSET = 1024

COL_SIZES = (CONV_WIDTH, CONV_WIDTH, CONV_WIDTH,
             ATTN_WIDTH, KV_WIDTH, KV_WIDTH, ATTN_WIDTH,
             D_MODEL, D_MODEL)
IN_WIDTH = sum(COL_SIZES)

kernel_name = "hybrid_conformer_conv_swa_sink_gated_merge"


def rmsnorm(x, g):
    xf = x.astype(jnp.float32)
    y = xf * lax.rsqrt(jnp.mean(xf * xf, axis=-1, keepdims=True) + EPS)
    return (y * g.astype(jnp.float32)).astype(x.dtype)


def layernorm(x, g, b):
    xf = x.astype(jnp.float32)
    mu = jnp.mean(xf, axis=-1, keepdims=True)
    var = jnp.mean(jnp.square(xf - mu), axis=-1, keepdims=True)
    y = (xf - mu) * lax.rsqrt(var + EPS)
    return (y * g.astype(jnp.float32) + b.astype(jnp.float32)).astype(x.dtype)


def split_columns(z):
    outs, start = [], 0
    for size in COL_SIZES:
        outs.append(z[..., start:start + size])
        start += size
    return outs


def conformer_conv(val, glu_gate, w_dw, b_dw, ln_g, ln_b, w_pw):
    u = val * jax.nn.sigmoid(glu_gate)
    c = lax.conv_general_dilated(
        u, w_dw[:, None, :].astype(u.dtype), window_strides=(1,),
        padding=[(CONV_KERNEL - 1, 0)],
        dimension_numbers=('NWC', 'WIO', 'NWC'),
        feature_group_count=CONV_WIDTH) + b_dw
    c = jax.nn.silu(layernorm(c, ln_g, ln_b))
    return c @ w_pw


def rope_tables(positions, dtype):
    inv = jnp.power(ROPE_THETA, -jnp.arange(0, ROPE_DIM, 2, dtype=jnp.float32) / ROPE_DIM)
    ang = positions.astype(jnp.float32)[..., None] * inv
    return jnp.cos(ang)[:, :, None, :].astype(dtype), jnp.sin(ang)[:, :, None, :].astype(dtype)


def partial_rope(t, cos, sin):
    half = ROPE_DIM // 2
    t1 = t[..., :half]
    t2 = t[..., half:ROPE_DIM]
    return jnp.concatenate([t1 * cos - t2 * sin, t2 * cos + t1 * sin, t[..., ROPE_DIM:]], axis=-1)


def sliding_window_sink_attention(q, k, v, sinks, positions):
    B, S = q.shape[0], q.shape[1]
    nb = S // BLOCK
    q = q.reshape(B, S, N_HEADS, HEAD_DIM)
    k = k.reshape(B, S, N_KV_HEADS, HEAD_DIM)
    v = v.reshape(B, S, N_KV_HEADS, HEAD_DIM)
    cos, sin = rope_tables(positions, q.dtype)
    q = partial_rope(q, cos, sin)
    k = partial_rope(k, cos, sin)
    q = q.reshape(B, nb, BLOCK, N_KV_HEADS, GROUP, HEAD_DIM)
    k = k.reshape(B, nb, BLOCK, N_KV_HEADS, HEAD_DIM)
    v = v.reshape(B, nb, BLOCK, N_KV_HEADS, HEAD_DIM)
    pad = ((0, 0), (1, 0), (0, 0), (0, 0), (0, 0))
    kb = jnp.concatenate([jnp.pad(k[:, :-1], pad), k], axis=2)
    vb = jnp.concatenate([jnp.pad(v[:, :-1], pad), v], axis=2)
    s = jnp.einsum('bnqkgd,bnskd->bnkgqs', q, kb).astype(jnp.float32) * (HEAD_DIM ** -0.5)
    qi = jnp.arange(BLOCK)[:, None]
    sj = jnp.arange(2 * BLOCK)[None, :]
    band = (sj <= qi + BLOCK) & (sj > qi + BLOCK - WINDOW)
    blk = jnp.arange(nb)[:, None, None]
    mask = band[None] & ((blk > 0) | (sj[None] >= BLOCK))
    s = jnp.where(mask[None, :, None, None], s, -jnp.inf)
    sink = jnp.broadcast_to(sinks.astype(jnp.float32).reshape(1, 1, N_KV_HEADS, GROUP, 1, 1),
                            s.shape[:-1] + (1,))
    probs = jax.nn.softmax(jnp.concatenate([s, sink], axis=-1), axis=-1)[..., :-1]
    o = jnp.einsum('bnkgqs,bnskd->bnqkgd', probs.astype(vb.dtype), vb)
    return o.reshape(B, S, ATTN_WIDTH)


def setup_inputs(seed: int = 0) -> dict:
    key = jax.random.key(seed)
    ks = jax.random.split(key, 20)
    f32 = jnp.float32
    x = jax.random.normal(ks[0], (BATCH, SEQ, D_MODEL), f32)
    p = jax.random.normal(ks[1], (DEPTH, BATCH, SEQ, PLE_DIM), f32)
    offsets = jax.random.randint(ks[2], (BATCH, 1), 0, MAX_POS_OFFSET, dtype=jnp.int32)
    positions = offsets + jnp.arange(SEQ, dtype=jnp.int32)[None, :]
    w_in = jax.random.normal(ks[3], (DEPTH, D_MODEL, IN_WIDTH), f32) * D_MODEL ** -0.5
    ln_pre = 1.0 + 0.05 * jax.random.normal(ks[4], (DEPTH, D_MODEL), f32)
    ln_post = 1.0 + 0.05 * jax.random.normal(ks[5], (DEPTH, D_MODEL), f32)
    w_dw = jax.random.normal(ks[6], (DEPTH, CONV_KERNEL, CONV_WIDTH), f32) * CONV_KERNEL ** -0.5
    b_dw = 0.02 * jax.random.normal(ks[7], (DEPTH, CONV_WIDTH), f32)
    conv_ln_g = 1.0 + 0.05 * jax.random.normal(ks[8], (DEPTH, CONV_WIDTH), f32)
    conv_ln_b = 0.02 * jax.random.normal(ks[9], (DEPTH, CONV_WIDTH), f32)
    w_pw = jax.random.normal(ks[10], (DEPTH, CONV_WIDTH, CONV_WIDTH), f32) * CONV_WIDTH ** -0.5
    sinks = 0.5 * jax.random.normal(ks[11], (DEPTH, N_HEADS), f32)
    w_br_conv = jax.random.normal(ks[12], (DEPTH, CONV_WIDTH, D_MODEL), f32) * CONV_WIDTH ** -0.5
    w_br_attn = jax.random.normal(ks[13], (DEPTH, ATTN_WIDTH, D_MODEL), f32) * ATTN_WIDTH ** -0.5
    w_out = jax.random.normal(ks[14], (DEPTH, D_MODEL, D_MODEL), f32) * D_MODEL ** -0.5
    w_ple_gate = jax.random.normal(ks[15], (DEPTH, D_MODEL, D_MODEL), f32) * D_MODEL ** -0.5
    w_ple_proj = jax.random.normal(ks[16], (DEPTH, PLE_DIM, D_MODEL), f32) * PLE_DIM ** -0.5
    return {"x": x, "p": p, "positions": positions, "w_in": w_in, "ln_pre": ln_pre,
            "ln_post": ln_post, "w_dw": w_dw, "b_dw": b_dw, "conv_ln_g": conv_ln_g,
            "conv_ln_b": conv_ln_b, "w_pw": w_pw, "sinks": sinks, "w_br_conv": w_br_conv,
            "w_br_attn": w_br_attn, "w_out": w_out, "w_ple_gate": w_ple_gate,
            "w_ple_proj": w_ple_proj}


def reference(x, p, positions, w_in, ln_pre, ln_post, w_dw, b_dw, conv_ln_g, conv_ln_b,
              w_pw, sinks, w_br_conv, w_br_attn, w_out, w_ple_gate, w_ple_proj):
    for i in range(DEPTH):
        h = rmsnorm(x, ln_pre[i])
        z = h @ w_in[i]
        (c_val, c_glu, c_gate, q, k, v, a_gate, g_conv, g_attn) = split_columns(z)
        ya = conformer_conv(c_val, c_glu, w_dw[i], b_dw[i], conv_ln_g[i], conv_ln_b[i], w_pw[i])
        ya = (ya * jax.nn.silu(c_gate)) @ w_br_conv[i]
        yb = sliding_window_sink_attention(q, k, v, sinks[i], positions)
        yb = (yb * jax.nn.silu(a_gate)) @ w_br_attn[i]
        m = jax.nn.sigmoid(g_conv) * ya + jax.nn.sigmoid(g_attn) * yb
        x = x + rmsnorm(m @ w_out[i], ln_post[i])
        x = x + jax.nn.sigmoid(x @ w_ple_gate[i]) * (p[i] @ w_ple_proj[i])
    return x
```

```python
import functools

import jax
import jax.numpy as jnp
from jax import lax
from jax.experimental import pallas as pl
from jax.experimental.pallas import tpu as pltpu

D_MODEL = 1024
PLE_DIM = 256
N_HEADS = 16
N_KV_HEADS = 2
HEAD_DIM = 64
GROUP = N_HEADS // N_KV_HEADS
ATTN_WIDTH = N_HEADS * HEAD_DIM
KV_WIDTH = N_KV_HEADS * HEAD_DIM
CONV_WIDTH = D_MODEL
CONV_KERNEL = 31
BLOCK = 128
ROPE_DIM = HEAD_DIM // 4
ROPE_THETA = 500000.0
EPS = 1e-6

COL_VAL = 0
COL_GLU = COL_VAL + CONV_WIDTH
COL_CGATE = COL_GLU + CONV_WIDTH
COL_Q = COL_CGATE + CONV_WIDTH
COL_K = COL_Q + ATTN_WIDTH
COL_V = COL_K + KV_WIDTH
COL_AGATE = COL_V + KV_WIDTH
COL_GCONV = COL_AGATE + ATTN_WIDTH
COL_GATTN = COL_GCONV + D_MODEL
IN_WIDTH = COL_GATTN + D_MODEL

V7X_LANES = 128
V7X_SUBLANES = 8
V7X_VMEM_BYTES = 64 * 1024 * 1024

TOKENS_PER_STEP = 256
CONV_HALO = 32
CONV_ROWS = 64
PAIRS_PER_GROUP = GROUP // 2
MASKED = -1e30

assert CONV_HALO >= CONV_KERNEL - 1 and CONV_HALO % V7X_SUBLANES == 0
assert TOKENS_PER_STEP % BLOCK == 0 and TOKENS_PER_STEP % CONV_ROWS == 0
assert 2 * HEAD_DIM == V7X_LANES and KV_WIDTH == V7X_LANES


def _sigmoid(t):
    return 1.0 / (1.0 + jnp.exp(-t))


def _silu(t):
    return t * _sigmoid(t)


def _rmsnorm(t, gain):
    return t * lax.rsqrt(jnp.mean(t * t, axis=-1, keepdims=True) + EPS) * gain


def _layer_kernel(sinks_ref, x_ref, p_ref, pos_ref, rope_ref, w_in_ref, ln_pre_ref, ln_post_ref,
                  w_dw_ref, b_dw_ref, cln_g_ref, cln_b_ref, w_pw_ref, w_brc_ref, w_bra_ref,
                  w_out_ref, w_pg_ref, w_pp_ref, out_ref,
                  u_buf, c_buf, q_buf, kb_buf, vb_buf, yb_buf):
    tokens = x_ref.shape[0]
    seq_step = pl.program_id(1)
    bf16 = jnp.bfloat16
    f32 = jnp.float32

    @pl.when(seq_step == 0)
    def _reset_carry():
        u_buf[0:CONV_HALO, :] = jnp.zeros((CONV_HALO, CONV_WIDTH), f32)
        kb_buf[:, :, 0:BLOCK, :] = jnp.zeros((N_KV_HEADS, 2, BLOCK, V7X_LANES), bf16)
        vb_buf[:, :, 0:BLOCK, :] = jnp.zeros((N_KV_HEADS, 2, BLOCK, V7X_LANES), bf16)

    def proj(act, w_ref, lo, hi):
        return jnp.dot(act, w_ref[:, lo:hi], preferred_element_type=f32)

    x = x_ref[...]
    h = _rmsnorm(x, ln_pre_ref[...]).astype(bf16)

    u = proj(h, w_in_ref, COL_VAL, COL_GLU) * _sigmoid(proj(h, w_in_ref, COL_GLU, COL_CGATE))
    u_buf[CONV_HALO:CONV_HALO + tokens, :] = u
    first_tap_row = CONV_HALO - (CONV_KERNEL - 1)
    for r0 in range(0, tokens, CONV_ROWS):
        for l0 in range(0, CONV_WIDTH, V7X_LANES):
            acc = jnp.zeros((CONV_ROWS, V7X_LANES), f32)
            for tap in range(CONV_KERNEL):
                row = first_tap_row + tap + r0
                acc = acc + w_dw_ref[tap:tap + 1, l0:l0 + V7X_LANES] * u_buf[row:row + CONV_ROWS, l0:l0 + V7X_LANES]
            c_buf[r0:r0 + CONV_ROWS, l0:l0 + V7X_LANES] = acc
    u_buf[0:CONV_HALO, :] = u_buf[tokens:tokens + CONV_HALO, :]
    c = c_buf[...] + b_dw_ref[...]
    mu = jnp.mean(c, axis=-1, keepdims=True)
    cc = c - mu
    var = jnp.mean(cc * cc, axis=-1, keepdims=True)
    c = cc * lax.rsqrt(var + EPS) * cln_g_ref[...] + cln_b_ref[...]
    pw = proj(_silu(c).astype(bf16), w_pw_ref, 0, CONV_WIDTH)
    ya = proj((pw * _silu(proj(h, w_in_ref, COL_CGATE, COL_Q))).astype(bf16), w_brc_ref, 0, D_MODEL)

    ang = pos_ref[...] * rope_ref[0:1, :]
    cos_t = jnp.cos(ang)
    sin_t = jnp.sin(ang)
    sin_lo = -sin_t * rope_ref[1:2, :]
    sin_hi = sin_t * rope_ref[2:3, :]

    def rope(t, scale):
        up = pltpu.roll(t, V7X_LANES - ROPE_DIM // 2, 1)
        down = pltpu.roll(t, ROPE_DIM // 2, 1)
        return (t * cos_t + up * sin_lo + down * sin_hi) * scale

    for l0 in range(0, ATTN_WIDTH, V7X_LANES):
        qc = proj(h, w_in_ref, COL_Q + l0, COL_Q + l0 + V7X_LANES)
        q_buf[:, l0:l0 + V7X_LANES] = rope(qc, HEAD_DIM ** -0.5).astype(bf16)

    k = rope(proj(h, w_in_ref, COL_K, COL_V), 1.0)
    v = proj(h, w_in_ref, COL_V, COL_AGATE)
    low_half = lax.broadcasted_iota(jnp.int32, (tokens, V7X_LANES), 1) < HEAD_DIM
    zero = jnp.zeros((tokens, V7X_LANES), f32)
    for src, dst in ((k, kb_buf), (v, vb_buf)):
        swapped = pltpu.roll(src, HEAD_DIM, 1)
        dst[0, 0, BLOCK:, :] = jnp.where(low_half, src, zero).astype(bf16)
        dst[0, 1, BLOCK:, :] = jnp.where(low_half, zero, swapped).astype(bf16)
        dst[1, 0, BLOCK:, :] = jnp.where(low_half, swapped, zero).astype(bf16)
        dst[1, 1, BLOCK:, :] = jnp.where(low_half, zero, src).astype(bf16)

    q_pos = lax.broadcasted_iota(jnp.int32, (BLOCK, 2 * BLOCK), 0)
    k_pos = lax.broadcasted_iota(jnp.int32, (BLOCK, 2 * BLOCK), 1)
    in_band = jnp.logical_and(k_pos <= q_pos + BLOCK, k_pos > q_pos)
    band_bias = jnp.where(in_band, 0.0, MASKED).astype(f32)
    first_bias = jnp.where(jnp.logical_and(in_band, k_pos >= BLOCK), 0.0, MASKED).astype(f32)
    low_half_blk = lax.broadcasted_iota(jnp.int32, (BLOCK, V7X_LANES), 1) < HEAD_DIM

    for blk in range(tokens // BLOCK):
        rows = slice(blk * BLOCK, (blk + 1) * BLOCK)
        kv_rows = slice(blk * BLOCK, (blk + 2) * BLOCK)
        if blk == 0:
            bias = jnp.where(seq_step == 0, first_bias, band_bias)
        else:
            bias = band_bias
        for g in range(N_KV_HEADS):
            k_blockdiag = jnp.concatenate([kb_buf[g, 0, kv_rows, :], kb_buf[g, 1, kv_rows, :]], axis=0)
            v_blockdiag = jnp.concatenate([vb_buf[g, 0, kv_rows, :], vb_buf[g, 1, kv_rows, :]], axis=0)
            for j in range(PAIRS_PER_GROUP):
                pair = g * PAIRS_PER_GROUP + j
                lanes = slice(pair * V7X_LANES, (pair + 1) * V7X_LANES)
                scores = lax.dot_general(q_buf[rows, lanes], k_blockdiag, (((1,), (1,)), ((), ())),
                                         preferred_element_type=f32)
                exps, inv_den = [], []
                for half in range(2):
                    sink = sinks_ref[2 * pair + half]
                    s_h = scores[:, half * 2 * BLOCK:(half + 1) * 2 * BLOCK] + bias
                    m = jnp.maximum(jnp.max(s_h, axis=-1, keepdims=True), sink)
                    e = jnp.exp(s_h - m)
                    den = jnp.sum(e, axis=-1, keepdims=True) + jnp.exp(sink - m)
                    exps.append(e.astype(bf16))
                    inv_den.append(1.0 / den)
                o = jnp.dot(jnp.concatenate(exps, axis=1), v_blockdiag, preferred_element_type=f32)
                yb_buf[rows, lanes] = o * jnp.where(low_half_blk, inv_den[0], inv_den[1])
    for dst in (kb_buf, vb_buf):
        dst[:, :, 0:BLOCK, :] = dst[:, :, tokens:tokens + BLOCK, :]

    yb = proj((yb_buf[...] * _silu(proj(h, w_in_ref, COL_AGATE, COL_GCONV))).astype(bf16), w_bra_ref, 0, D_MODEL)

    merged = (_sigmoid(proj(h, w_in_ref, COL_GCONV, COL_GATTN)) * ya
              + _sigmoid(proj(h, w_in_ref, COL_GATTN, IN_WIDTH)) * yb)
    x1 = x + _rmsnorm(proj(merged.astype(bf16), w_out_ref, 0, D_MODEL), ln_post_ref[...])
    gate = _sigmoid(proj(x1.astype(bf16), w_pg_ref, 0, D_MODEL))
    out_ref[...] = x1 + gate * proj(p_ref[...].astype(bf16), w_pp_ref, 0, D_MODEL)


def _rope_lane_table():
    half = ROPE_DIM // 2
    inv = jnp.power(ROPE_THETA, -jnp.arange(0, ROPE_DIM, 2, dtype=jnp.float32) / ROPE_DIM)
    dim = jnp.arange(V7X_LANES) % HEAD_DIM
    inv_lane = jnp.where(dim < ROPE_DIM, inv[dim % half], 0.0)
    first = (dim < half).astype(jnp.float32)
    second = jnp.logical_and(dim >= half, dim < ROPE_DIM).astype(jnp.float32)
    pad = jnp.zeros((V7X_SUBLANES - 3, V7X_LANES), jnp.float32)
    return jnp.concatenate([inv_lane[None], first[None], second[None], pad], axis=0)


def _resident(shape):
    return pl.BlockSpec(shape, lambda b, s: (0,) * len(shape), pipeline_mode=pl.Buffered(1))


def _layer(x, p, pos, w_in, ln_pre, ln_post, w_dw, b_dw, cln_g, cln_b, w_pw, sinks,
           w_brc, w_bra, w_out, w_pg, w_pp):
    batch, seq, _ = x.shape
    t = TOKENS_PER_STEP
    assert seq % t == 0
    bf16 = jnp.bfloat16
    row = lambda a: a.reshape(1, -1).astype(jnp.float32)
    tile = lambda width: pl.BlockSpec((None, t, width), lambda b, s: (b, s, 0))
    weights = [w.astype(bf16) for w in (w_pw, w_brc, w_bra, w_out, w_pg, w_pp)]
    operands = [sinks.astype(jnp.float32), x, p, pos.astype(jnp.float32)[..., None], _rope_lane_table(),
                w_in.astype(bf16), row(ln_pre), row(ln_post), w_dw.astype(jnp.float32), row(b_dw),
                row(cln_g), row(cln_b)] + weights
    in_specs = [pl.BlockSpec(memory_space=pltpu.SMEM), tile(D_MODEL), tile(PLE_DIM), tile(1)]
    in_specs += [_resident(a.shape) for a in operands[4:]]
    scratch = [
        pltpu.VMEM((CONV_HALO + t, CONV_WIDTH), jnp.float32),
        pltpu.VMEM((t, CONV_WIDTH), jnp.float32),
        pltpu.VMEM((t, ATTN_WIDTH), bf16),
        pltpu.VMEM((N_KV_HEADS, 2, BLOCK + t, V7X_LANES), bf16),
        pltpu.VMEM((N_KV_HEADS, 2, BLOCK + t, V7X_LANES), bf16),
        pltpu.VMEM((t, ATTN_WIDTH), jnp.float32),
    ]
    return pl.pallas_call(
        _layer_kernel,
        grid=(batch, seq // t),
        in_specs=in_specs,
        out_specs=tile(D_MODEL),
        out_shape=jax.ShapeDtypeStruct(x.shape, x.dtype),
        scratch_shapes=scratch,
        compiler_params=pltpu.CompilerParams(
            dimension_semantics=("arbitrary", "arbitrary"),
            vmem_limit_bytes=V7X_VMEM_BYTES - 8 * 1024 * 1024),
        name="hybrid_layer",
    )(*operands)


def kernel(x, p, positions, w_in, ln_pre, ln_post, w_dw, b_dw, conv_ln_g, conv_ln_b, w_pw, sinks,
           w_br_conv, w_br_attn, w_out, w_ple_gate, w_ple_proj):
    for i in range(w_in.shape[0]):
        x = _layer(x, p[i], positions, w_in[i], ln_pre[i], ln_post[i], w_dw[i], b_dw[i], conv_ln_g[i],
                   conv_ln_b[i], w_pw[i], sinks[i], w_br_conv[i], w_br_attn[i], w_out[i], w_ple_gate[i],
                   w_ple_proj[i])
    return x
```

```python
import jax
import jax.numpy as jnp
from jax import lax
from jax.experimental import pallas as pl
from jax.experimental.pallas import tpu as pltpu

D_MODEL = 1024
PLE_DIM = 256
N_HEADS = 16
N_KV_HEADS = 2
HEAD_DIM = 64
GROUP = N_HEADS // N_KV_HEADS
ATTN_WIDTH = N_HEADS * HEAD_DIM
KV_WIDTH = N_KV_HEADS * HEAD_DIM
CONV_WIDTH = D_MODEL
CONV_KERNEL = 31
BLOCK = 128
ROPE_DIM = HEAD_DIM // 4
ROPE_THETA = 500000.0
EPS = 1e-6

COL_VAL = 0
COL_GLU = COL_VAL + CONV_WIDTH
COL_CGATE = COL_GLU + CONV_WIDTH
COL_Q = COL_CGATE + CONV_WIDTH
COL_K = COL_Q + ATTN_WIDTH
COL_V = COL_K + KV_WIDTH
COL_AGATE = COL_V + KV_WIDTH
COL_GCONV = COL_AGATE + ATTN_WIDTH
COL_GATTN = COL_GCONV + D_MODEL
IN_WIDTH = COL_GATTN + D_MODEL

V7X_LANES = 128
V7X_SUBLANES = 8
V7X_MXU_WIDTH = 256
V7X_VMEM_BYTES = 64 * 1024 * 1024

TOKENS_PER_STEP = 256
CONV_HALO = 32
CONV_ROWS = 64
LOOP_UNROLL = 8
PAIRS = N_HEADS // 2
PAIRS_PER_GROUP = GROUP // 2
MASKED = -1e30

SLAB = V7X_MXU_WIDTH
Z_CGATE = 0
Z_Q = COL_Q - COL_CGATE
Z_KV = COL_K - COL_CGATE
Z_AGATE = COL_AGATE - COL_CGATE
Z_GCONV = COL_GCONV - COL_CGATE
Z_GATTN = COL_GATTN - COL_CGATE
Z_PW = IN_WIDTH - COL_CGATE
Z_WIDTH = Z_PW + CONV_WIDTH
IN_SLABS = Z_PW // SLAB
ALL_SLABS = Z_WIDTH // SLAB

assert CONV_HALO >= CONV_KERNEL - 1 and CONV_HALO % V7X_SUBLANES == 0
assert TOKENS_PER_STEP % BLOCK == 0 and (TOKENS_PER_STEP // 2) % CONV_ROWS == 0
assert 2 * HEAD_DIM == V7X_LANES and KV_WIDTH == V7X_LANES
assert Z_PW % SLAB == 0 and CONV_WIDTH % SLAB == 0


def _sigmoid(t):
    return 1.0 / (1.0 + jnp.exp(-t))


def _silu(t):
    return t * _sigmoid(t)


def _rmsnorm(t, gain):
    return t * lax.rsqrt(jnp.mean(t * t, axis=-1, keepdims=True) + EPS) * gain


def _layer_kernel(sinks_ref, x_ref, p_ref, pos_ref, rope_ref, w_cat_ref, ln_pre_ref, ln_post_ref,
                  w_dw_ref, b_dw_ref, cln_g_ref, cln_b_ref, w_brc_ref, w_bra_ref,
                  w_out_ref, w_pg_ref, w_pp_ref, out_ref,
                  lhs_buf, u_buf, c_buf, z_buf, q_buf, kb_buf, vb_buf, bias_buf, yb_buf):
    tokens = x_ref.shape[0]
    seq_step = pl.program_id(1)
    bf16 = jnp.bfloat16
    f32 = jnp.float32
    n_conv_units = 2 * (CONV_WIDTH // V7X_LANES)
    n_attn_units = (tokens // BLOCK) * PAIRS
    assert n_conv_units <= IN_SLABS and ALL_SLABS - n_conv_units <= n_attn_units

    @pl.when(seq_step == 0)
    def _reset_carry():
        u_buf[0:CONV_HALO, :] = jnp.zeros((CONV_HALO, CONV_WIDTH), f32)
        kb_buf[:, :, 0:BLOCK, :] = jnp.zeros((N_KV_HEADS, 2, BLOCK, V7X_LANES), bf16)
        vb_buf[:, :, 0:BLOCK, :] = jnp.zeros((N_KV_HEADS, 2, BLOCK, 2 * V7X_LANES), bf16)

    @pl.when(seq_step > 0)
    def _shift_carry():
        u_buf[0:CONV_HALO, :] = u_buf[tokens:tokens + CONV_HALO, :]
        for buf in (kb_buf, vb_buf):
            buf[:, :, 0:BLOCK, :] = buf[:, :, tokens:tokens + BLOCK, :]

    def slab_projection(slab, lhs_index):
        col = slab * SLAB if isinstance(slab, int) else pl.multiple_of(slab * SLAB, SLAB)
        z_buf[:, pl.ds(col, SLAB)] = jnp.dot(lhs_buf[lhs_index], w_cat_ref[:, pl.ds(COL_CGATE + col, SLAB)],
                                             preferred_element_type=f32)

    x = x_ref[...]
    h = _rmsnorm(x, ln_pre_ref[...]).astype(bf16)
    lhs_buf[0] = h

    ang = pos_ref[...] * rope_ref[0:1, :]
    cos_t = jnp.cos(ang)
    sin_t = jnp.sin(ang)
    sin_lo = -sin_t * rope_ref[1:2, :]
    sin_hi = sin_t * rope_ref[2:3, :]

    def rope(t, scale):
        up = pltpu.roll(t, V7X_LANES - ROPE_DIM // 2, 1)
        down = pltpu.roll(t, ROPE_DIM // 2, 1)
        return (t * cos_t + up * sin_lo + down * sin_hi) * scale

    val = jnp.dot(h, w_cat_ref[:, COL_VAL:COL_GLU], preferred_element_type=f32)
    glu = jnp.dot(h, w_cat_ref[:, COL_GLU:COL_CGATE], preferred_element_type=f32)
    u_buf[CONV_HALO:CONV_HALO + tokens, :] = val * _sigmoid(glu)
    first_tap_row = CONV_HALO - (CONV_KERNEL - 1)

    def conv_chunk(r0, l0):
        lanes = pl.ds(l0, V7X_LANES)
        out = None
        for shift in range(V7X_SUBLANES):
            rows = CONV_ROWS if shift == 0 else CONV_ROWS + V7X_SUBLANES
            part = None
            for a in range(CONV_HALO // V7X_SUBLANES + 1):
                tap = V7X_SUBLANES * a + shift - first_tap_row
                if not 0 <= tap < CONV_KERNEL:
                    continue
                row = pl.multiple_of(r0 + V7X_SUBLANES * a, V7X_SUBLANES)
                term = w_dw_ref[tap:tap + 1, lanes] * u_buf[pl.ds(row, rows), lanes]
                part = term if part is None else part + term
            part = part if shift == 0 else part[shift:shift + CONV_ROWS, :]
            out = part if out is None else out + part
        return out

    def conv_unit(unit, carry):
        l0 = pl.multiple_of(lax.shift_right_logical(unit, 1) * V7X_LANES, V7X_LANES)
        half_rows = tokens // 2
        r_base = lax.bitwise_and(unit, 1) * half_rows
        for rc in range(half_rows // CONV_ROWS):
            r0 = pl.multiple_of(r_base + rc * CONV_ROWS, CONV_ROWS)
            c_buf[pl.ds(r0, CONV_ROWS), pl.ds(l0, V7X_LANES)] = conv_chunk(r0, l0)
        slab_projection(unit, 0)
        return carry

    lax.fori_loop(0, n_conv_units, conv_unit, 0, unroll=LOOP_UNROLL)

    c = c_buf[...] + b_dw_ref[...]
    mu = jnp.mean(c, axis=-1, keepdims=True)
    cc = c - mu
    var = jnp.mean(cc * cc, axis=-1, keepdims=True)
    c = cc * lax.rsqrt(var + EPS) * cln_g_ref[...] + cln_b_ref[...]
    lhs_buf[1] = _silu(c).astype(bf16)

    for l0 in range(0, ATTN_WIDTH, V7X_LANES):
        q_buf[:, l0:l0 + V7X_LANES] = rope(z_buf[:, Z_Q + l0:Z_Q + l0 + V7X_LANES], HEAD_DIM ** -0.5).astype(bf16)
    k = rope(z_buf[:, Z_KV:Z_KV + KV_WIDTH], 1.0)
    v = z_buf[:, Z_KV + KV_WIDTH:Z_AGATE]
    lane = lax.broadcasted_iota(jnp.int32, (tokens, V7X_LANES), 1)
    low_half = lane < HEAD_DIM
    zero = jnp.zeros((tokens, V7X_LANES), f32)
    new_rows = slice(BLOCK, BLOCK + tokens)
    for src, dst in ((k, kb_buf), (v, vb_buf)):
        swapped = pltpu.roll(src, HEAD_DIM, 1)
        dst[0, 0, new_rows, 0:V7X_LANES] = jnp.where(low_half, src, zero).astype(bf16)
        dst[0, 1, new_rows, 0:V7X_LANES] = jnp.where(low_half, zero, swapped).astype(bf16)
        dst[1, 0, new_rows, 0:V7X_LANES] = jnp.where(low_half, swapped, zero).astype(bf16)
        dst[1, 1, new_rows, 0:V7X_LANES] = jnp.where(low_half, zero, src).astype(bf16)
    ones_lo = jnp.where(low_half, 1.0, 0.0).astype(bf16)
    ones_hi = jnp.where(low_half, 0.0, 1.0).astype(bf16)
    for g in range(N_KV_HEADS):
        vb_buf[g, 0, new_rows, V7X_LANES:] = ones_lo
        vb_buf[g, 1, new_rows, V7X_LANES:] = ones_hi

    q_pos = lax.broadcasted_iota(jnp.int32, (BLOCK, 2 * BLOCK), 0)
    k_pos = lax.broadcasted_iota(jnp.int32, (BLOCK, 2 * BLOCK), 1)
    in_band = jnp.logical_and(k_pos <= q_pos + BLOCK, k_pos > q_pos)
    bias_buf[0] = jnp.where(in_band, 0.0, MASKED).astype(f32)
    bias_buf[1] = jnp.where(jnp.logical_and(in_band, k_pos >= BLOCK), 0.0, MASKED).astype(f32)
    low_half_blk = lax.broadcasted_iota(jnp.int32, (BLOCK, V7X_LANES), 1) < HEAD_DIM

    def attn_unit(unit):
        blk, pair = divmod(unit, PAIRS)
        g = pair // PAIRS_PER_GROUP
        r0 = blk * BLOCK
        l0 = pair * V7X_LANES
        window = pl.ds(r0, 2 * BLOCK)
        k_blockdiag = jnp.concatenate([kb_buf[g, 0, window, :], kb_buf[g, 1, window, :]], axis=0)
        v_blockdiag = jnp.concatenate([vb_buf[g, 0, window, :], vb_buf[g, 1, window, :]], axis=0)
        scores = lax.dot_general(q_buf[pl.ds(r0, BLOCK), pl.ds(l0, V7X_LANES)], k_blockdiag,
                                 (((1,), (1,)), ((), ())), preferred_element_type=f32)
        bias = bias_buf[0] if blk > 0 else jnp.where(seq_step == 0, bias_buf[1], bias_buf[0])
        exps, sink_terms = [], []
        for half in range(2):
            sink = sinks_ref[2 * pair + half]
            s_h = scores[:, half * 2 * BLOCK:(half + 1) * 2 * BLOCK] + bias
            m = jnp.maximum(jnp.max(s_h, axis=-1, keepdims=True), sink)
            exps.append(jnp.exp(s_h - m).astype(bf16))
            sink_terms.append(jnp.exp(sink - m))
        o = jnp.dot(jnp.concatenate(exps, axis=1), v_blockdiag, preferred_element_type=f32)
        den = o[:, V7X_LANES:] + jnp.where(low_half_blk, sink_terms[0], sink_terms[1])
        yb_buf[pl.ds(r0, BLOCK), pl.ds(l0, V7X_LANES)] = o[:, :V7X_LANES] * (1.0 / den)

    for unit in range(n_attn_units):
        attn_unit(unit)
        slab = n_conv_units + unit
        if slab < ALL_SLABS:
            slab_projection(slab, 0 if slab < IN_SLABS else 1)

    def proj(act, w_ref):
        return jnp.dot(act.astype(bf16), w_ref[...], preferred_element_type=f32)

    zcol = lambda lo, width: z_buf[:, lo:lo + width]
    ya = proj(zcol(Z_PW, CONV_WIDTH) * _silu(zcol(Z_CGATE, CONV_WIDTH)), w_brc_ref)
    yb = proj(yb_buf[...] * _silu(zcol(Z_AGATE, ATTN_WIDTH)), w_bra_ref)

    merged = _sigmoid(zcol(Z_GCONV, D_MODEL)) * ya + _sigmoid(zcol(Z_GATTN, D_MODEL)) * yb
    x1 = x + _rmsnorm(proj(merged, w_out_ref), ln_post_ref[...])
    gate = _sigmoid(proj(x1, w_pg_ref))
    out_ref[...] = x1 + gate * proj(p_ref[...], w_pp_ref)


def _rope_lane_table():
    half = ROPE_DIM // 2
    inv = jnp.power(ROPE_THETA, -jnp.arange(0, ROPE_DIM, 2, dtype=jnp.float32) / ROPE_DIM)
    dim = jnp.arange(V7X_LANES) % HEAD_DIM
    inv_lane = jnp.where(dim < ROPE_DIM, inv[dim % half], 0.0)
    first = (dim < half).astype(jnp.float32)
    second = jnp.logical_and(dim >= half, dim < ROPE_DIM).astype(jnp.float32)
    pad = jnp.zeros((V7X_SUBLANES - 3, V7X_LANES), jnp.float32)
    return jnp.concatenate([inv_lane[None], first[None], second[None], pad], axis=0)


def _resident(shape):
    return pl.BlockSpec(shape, lambda b, s: (0,) * len(shape), pipeline_mode=pl.Buffered(1))


def _layer(x, p, pos, w_in, ln_pre, ln_post, w_dw, b_dw, cln_g, cln_b, w_pw, sinks,
           w_brc, w_bra, w_out, w_pg, w_pp):
    batch, seq, _ = x.shape
    t = TOKENS_PER_STEP
    assert seq % t == 0
    bf16 = jnp.bfloat16
    row = lambda a: a.reshape(1, -1).astype(jnp.float32)
    tile = lambda width: pl.BlockSpec((None, t, width), lambda b, s: (b, s, 0))
    w_cat = jnp.concatenate([w_in, w_pw], axis=1).astype(bf16)
    weights = [w.astype(bf16) for w in (w_brc, w_bra, w_out, w_pg, w_pp)]
    operands = [sinks.astype(jnp.float32), x, p, pos.astype(jnp.float32)[..., None], _rope_lane_table(),
                w_cat, row(ln_pre), row(ln_post), w_dw.astype(jnp.float32), row(b_dw),
                row(cln_g), row(cln_b)] + weights
    in_specs = [pl.BlockSpec(memory_space=pltpu.SMEM), tile(D_MODEL), tile(PLE_DIM), tile(1)]
    in_specs += [_resident(a.shape) for a in operands[4:]]
    scratch = [
        pltpu.VMEM((2, t, D_MODEL), bf16),
        pltpu.VMEM((CONV_HALO + t, CONV_WIDTH), jnp.float32),
        pltpu.VMEM((t, CONV_WIDTH), jnp.float32),
        pltpu.VMEM((t, Z_WIDTH), jnp.float32),
        pltpu.VMEM((t, ATTN_WIDTH), bf16),
        pltpu.VMEM((N_KV_HEADS, 2, BLOCK + t, V7X_LANES), bf16),
        pltpu.VMEM((N_KV_HEADS, 2, BLOCK + t, 2 * V7X_LANES), bf16),
        pltpu.VMEM((2, BLOCK, 2 * BLOCK), jnp.float32),
        pltpu.VMEM((t, ATTN_WIDTH), jnp.float32),
    ]
    return pl.pallas_call(
        _layer_kernel,
        grid=(batch, seq // t),
        in_specs=in_specs,
        out_specs=tile(D_MODEL),
        out_shape=jax.ShapeDtypeStruct(x.shape, x.dtype),
        scratch_shapes=scratch,
        compiler_params=pltpu.CompilerParams(
            dimension_semantics=("arbitrary", "arbitrary"),
            vmem_limit_bytes=V7X_VMEM_BYTES - 8 * 1024 * 1024),
        name="hybrid_layer",
    )(*operands)


def kernel(x, p, positions, w_in, ln_pre, ln_post, w_dw, b_dw, conv_ln_g, conv_ln_b, w_pw, sinks,
           w_br_conv, w_br_attn, w_out, w_ple_gate, w_ple_proj):
    for i in range(w_in.shape[0]):
        x = _layer(x, p[i], positions, w_in[i], ln_pre[i], ln_post[i], w_dw[i], b_dw[i], conv_ln_g[i],
                   conv_ln_b[i], w_pw[i], sinks[i], w_br_conv[i], w_br_attn[i], w_out[i], w_ple_gate[i],
                   w_ple_proj[i])
    return x
```

```python
import jax
import jax.numpy as jnp
from jax import lax
from jax.experimental import pallas as pl
from jax.experimental.pallas import tpu as pltpu

D_MODEL = 1024
PLE_DIM = 256
N_HEADS = 16
N_KV_HEADS = 2
HEAD_DIM = 64
GROUP = N_HEADS // N_KV_HEADS
ATTN_WIDTH = N_HEADS * HEAD_DIM
KV_WIDTH = N_KV_HEADS * HEAD_DIM
CONV_WIDTH = D_MODEL
CONV_KERNEL = 31
BLOCK = 128
ROPE_DIM = HEAD_DIM // 4
ROPE_THETA = 500000.0
EPS = 1e-6

COL_VAL = 0
COL_GLU = COL_VAL + CONV_WIDTH
COL_CGATE = COL_GLU + CONV_WIDTH
COL_Q = COL_CGATE + CONV_WIDTH
COL_K = COL_Q + ATTN_WIDTH
COL_V = COL_K + KV_WIDTH
COL_AGATE = COL_V + KV_WIDTH
COL_GCONV = COL_AGATE + ATTN_WIDTH
COL_GATTN = COL_GCONV + D_MODEL
IN_WIDTH = COL_GATTN + D_MODEL

V7X_LANES = 128
V7X_SUBLANES = 8
V7X_MXU_WIDTH = 256
V7X_VMEM_BYTES = 64 * 1024 * 1024

TOKENS_PER_STEP = 256
CONV_HALO = 32
CONV_ROWS = 64
LOOP_UNROLL = 8
PAIRS = N_HEADS // 2
PAIRS_PER_GROUP = GROUP // 2
MASKED = -1e30

SLAB = V7X_MXU_WIDTH
Z_CGATE = 0
Z_Q = COL_Q - COL_CGATE
Z_KV = COL_K - COL_CGATE
Z_AGATE = COL_AGATE - COL_CGATE
Z_GCONV = COL_GCONV - COL_CGATE
Z_GATTN = COL_GATTN - COL_CGATE
Z_WIDTH = IN_WIDTH - COL_CGATE
Z_PW = Z_Q
IN_SLABS = Z_WIDTH // SLAB
ALL_SLABS = IN_SLABS + CONV_WIDTH // SLAB

assert CONV_HALO >= CONV_KERNEL - 1 and CONV_HALO % V7X_SUBLANES == 0
assert TOKENS_PER_STEP % BLOCK == 0 and (TOKENS_PER_STEP // 2) % CONV_ROWS == 0
assert 2 * HEAD_DIM == V7X_LANES and KV_WIDTH == V7X_LANES
assert Z_WIDTH % SLAB == 0 and CONV_WIDTH % SLAB == 0 and CONV_WIDTH == ATTN_WIDTH


LOG2_E = 1.4426950408889634


def _sigmoid(t):
    return 1.0 / (1.0 + jnp.exp2(t * -LOG2_E))


def _silu(t):
    return t * _sigmoid(t)


def _rmsnorm(t, gain):
    return t * lax.rsqrt(jnp.mean(t * t, axis=-1, keepdims=True) + EPS) * gain


def _layer_kernel(sinks_ref, x_ref, p_ref, pos_ref, rope_ref, w_cat_ref, ln_pre_ref, ln_post_ref,
                  w_dw_ref, b_dw_ref, cln_g_ref, cln_b_ref, w_brc_ref, w_bra_ref,
                  w_out_ref, w_pg_ref, w_pp_ref, out_ref,
                  lhs_buf, u_buf, c_buf, z_buf, q_buf, kb_buf, vb_buf, bias_buf, yb_buf):
    tokens = x_ref.shape[0]
    seq_step = pl.program_id(1)
    bf16 = jnp.bfloat16
    f32 = jnp.float32
    n_conv_units = 2 * (CONV_WIDTH // V7X_LANES)
    n_attn_units = (tokens // BLOCK) * PAIRS
    assert n_conv_units <= IN_SLABS and ALL_SLABS - n_conv_units <= n_attn_units

    @pl.when(seq_step == 0)
    def _reset_carry():
        u_buf[0:CONV_HALO, :] = jnp.zeros((CONV_HALO, CONV_WIDTH), f32)
        kb_buf[:, :, 0:BLOCK, :] = jnp.zeros((N_KV_HEADS, 2, BLOCK, V7X_LANES), bf16)
        vb_buf[:, :, 0:BLOCK, :] = jnp.zeros((N_KV_HEADS, 2, BLOCK, 2 * V7X_LANES), bf16)

    @pl.when(seq_step > 0)
    def _shift_carry():
        u_buf[0:CONV_HALO, :] = u_buf[tokens:tokens + CONV_HALO, :]
        for buf in (kb_buf, vb_buf):
            buf[:, :, 0:BLOCK, :] = buf[:, :, tokens:tokens + BLOCK, :]

    def slab_projection(slab, lhs_index):
        if isinstance(slab, int):
            w_col = slab * SLAB
            z_col = w_col if slab < IN_SLABS else Z_PW + (slab - IN_SLABS) * SLAB
        else:
            w_col = z_col = pl.multiple_of(slab * SLAB, SLAB)
        z_buf[:, pl.ds(z_col, SLAB)] = jnp.dot(lhs_buf[lhs_index], w_cat_ref[:, pl.ds(COL_CGATE + w_col, SLAB)],
                                               preferred_element_type=f32)

    x = x_ref[...]
    h = _rmsnorm(x, ln_pre_ref[...]).astype(bf16)
    lhs_buf[0] = h

    ang = pos_ref[...] * rope_ref[0:1, :]
    cos_t = jnp.cos(ang)
    sin_t = jnp.sin(ang)
    sin_lo = -sin_t * rope_ref[1:2, :]
    sin_hi = sin_t * rope_ref[2:3, :]

    def rope(t, scale):
        up = pltpu.roll(t, V7X_LANES - ROPE_DIM // 2, 1)
        down = pltpu.roll(t, ROPE_DIM // 2, 1)
        return (t * cos_t + up * sin_lo + down * sin_hi) * scale

    val = jnp.dot(h, w_cat_ref[:, COL_VAL:COL_GLU], preferred_element_type=f32)
    glu = jnp.dot(h, w_cat_ref[:, COL_GLU:COL_CGATE], preferred_element_type=f32)
    u_buf[CONV_HALO:CONV_HALO + tokens, :] = val * _sigmoid(glu)
    first_tap_row = CONV_HALO - (CONV_KERNEL - 1)

    def conv_chunk(r0, l0):
        lanes = pl.ds(l0, V7X_LANES)
        window = u_buf[pl.ds(r0, CONV_ROWS + CONV_HALO), lanes]
        out = None
        for shift in range(V7X_SUBLANES):
            rows = CONV_ROWS if shift == 0 else CONV_ROWS + V7X_SUBLANES
            part = None
            for a in range(CONV_HALO // V7X_SUBLANES + 1):
                tap = V7X_SUBLANES * a + shift - first_tap_row
                if not 0 <= tap < CONV_KERNEL:
                    continue
                term = w_dw_ref[tap:tap + 1, lanes] * window[V7X_SUBLANES * a:V7X_SUBLANES * a + rows, :]
                part = term if part is None else part + term
            part = part if shift == 0 else part[shift:shift + CONV_ROWS, :]
            out = part if out is None else out + part
        return out

    def conv_unit(unit, carry):
        l0 = pl.multiple_of(lax.shift_right_logical(unit, 1) * V7X_LANES, V7X_LANES)
        half_rows = tokens // 2
        r_base = lax.bitwise_and(unit, 1) * half_rows
        for rc in range(half_rows // CONV_ROWS):
            r0 = pl.multiple_of(r_base + rc * CONV_ROWS, CONV_ROWS)
            c_buf[pl.ds(r0, CONV_ROWS), pl.ds(l0, V7X_LANES)] = conv_chunk(r0, l0)
        slab_projection(unit, 0)
        return carry

    lax.fori_loop(0, n_conv_units, conv_unit, 0, unroll=LOOP_UNROLL)

    c = c_buf[...] + b_dw_ref[...]
    mu = jnp.mean(c, axis=-1, keepdims=True)
    cc = c - mu
    var = jnp.mean(cc * cc, axis=-1, keepdims=True)
    c = cc * lax.rsqrt(var + EPS) * cln_g_ref[...] + cln_b_ref[...]
    lhs_buf[1] = _silu(c).astype(bf16)

    for l0 in range(0, ATTN_WIDTH, V7X_LANES):
        q_buf[:, l0:l0 + V7X_LANES] = rope(z_buf[:, Z_Q + l0:Z_Q + l0 + V7X_LANES],
                                           LOG2_E * HEAD_DIM ** -0.5).astype(bf16)
    k = rope(z_buf[:, Z_KV:Z_KV + KV_WIDTH], 1.0)
    v = z_buf[:, Z_KV + KV_WIDTH:Z_AGATE]
    lane = lax.broadcasted_iota(jnp.int32, (tokens, V7X_LANES), 1)
    low_half = lane < HEAD_DIM
    zero = jnp.zeros((tokens, V7X_LANES), f32)
    new_rows = slice(BLOCK, BLOCK + tokens)
    for src, dst in ((k, kb_buf), (v, vb_buf)):
        swapped = pltpu.roll(src, HEAD_DIM, 1)
        dst[0, 0, new_rows, 0:V7X_LANES] = jnp.where(low_half, src, zero).astype(bf16)
        dst[0, 1, new_rows, 0:V7X_LANES] = jnp.where(low_half, zero, swapped).astype(bf16)
        dst[1, 0, new_rows, 0:V7X_LANES] = jnp.where(low_half, swapped, zero).astype(bf16)
        dst[1, 1, new_rows, 0:V7X_LANES] = jnp.where(low_half, zero, src).astype(bf16)
    ones_lo = jnp.where(low_half, 1.0, 0.0).astype(bf16)
    ones_hi = jnp.where(low_half, 0.0, 1.0).astype(bf16)
    for g in range(N_KV_HEADS):
        vb_buf[g, 0, new_rows, V7X_LANES:] = ones_lo
        vb_buf[g, 1, new_rows, V7X_LANES:] = ones_hi

    q_pos = lax.broadcasted_iota(jnp.int32, (BLOCK, 2 * BLOCK), 0)
    k_pos = lax.broadcasted_iota(jnp.int32, (BLOCK, 2 * BLOCK), 1)
    in_band = jnp.logical_and(k_pos <= q_pos + BLOCK, k_pos > q_pos)
    bias_buf[0] = jnp.where(in_band, 0.0, MASKED).astype(f32)
    bias_buf[1] = jnp.where(jnp.logical_and(in_band, k_pos >= BLOCK), 0.0, MASKED).astype(f32)
    low_half_blk = lax.broadcasted_iota(jnp.int32, (BLOCK, V7X_LANES), 1) < HEAD_DIM

    def attn_unit(unit):
        if isinstance(unit, int):
            blk, pair = divmod(unit, PAIRS)
            g = pair // PAIRS_PER_GROUP
            r0 = blk * BLOCK
            l0 = pair * V7X_LANES
        else:
            blk = None
            pair = lax.bitwise_and(unit, PAIRS - 1)
            g = lax.shift_right_logical(pair, 2)
            r0 = pl.multiple_of(lax.shift_right_logical(unit, 3) * BLOCK, BLOCK)
            l0 = pl.multiple_of(pair * V7X_LANES, V7X_LANES)
        rows = pl.ds(r0, BLOCK)
        lanes = pl.ds(l0, V7X_LANES)
        window = pl.ds(r0, 2 * BLOCK)
        k_blockdiag = jnp.concatenate([kb_buf[g, 0, window, :], kb_buf[g, 1, window, :]], axis=0)
        v_blockdiag = jnp.concatenate([vb_buf[g, 0, window, :], vb_buf[g, 1, window, :]], axis=0)
        scores = lax.dot_general(q_buf[rows, lanes], k_blockdiag,
                                 (((1,), (1,)), ((), ())), preferred_element_type=f32)
        bias = jnp.where(seq_step == 0, bias_buf[1], bias_buf[0]) if blk == 0 else bias_buf[0]
        exps, sink_terms = [], []
        for half in range(2):
            sink = sinks_ref[2 * pair + half] * LOG2_E
            s_h = scores[:, half * 2 * BLOCK:(half + 1) * 2 * BLOCK] + bias
            m = jnp.maximum(jnp.max(s_h, axis=-1, keepdims=True), sink)
            exps.append(jnp.exp2(s_h - m).astype(bf16))
            sink_terms.append(jnp.exp2(sink - m))
        o = jnp.dot(jnp.concatenate(exps, axis=1), v_blockdiag, preferred_element_type=f32)
        den = o[:, V7X_LANES:] + jnp.where(low_half_blk, sink_terms[0], sink_terms[1])
        a_gate = z_buf[rows, pl.ds(Z_AGATE + l0, V7X_LANES)]
        yb_buf[rows, lanes] = (o[:, :V7X_LANES] * (1.0 / den) * _silu(a_gate)).astype(bf16)

    assert ALL_SLABS - n_conv_units <= PAIRS + 1
    for unit in range(PAIRS):
        attn_unit(unit)
        slab = n_conv_units + unit
        slab_projection(slab, 0 if slab < IN_SLABS else 1)
    for slab in range(n_conv_units + PAIRS, ALL_SLABS):
        slab_projection(slab, 0 if slab < IN_SLABS else 1)

    def later_block_unit(unit, carry):
        attn_unit(unit)
        return carry

    lax.fori_loop(PAIRS, n_attn_units, later_block_unit, 0, unroll=PAIRS)

    def proj(act, w_ref):
        return jnp.dot(act.astype(bf16), w_ref[...], preferred_element_type=f32)

    zcol = lambda lo, width: z_buf[:, lo:lo + width]
    ya = proj(zcol(Z_PW, CONV_WIDTH) * _silu(zcol(Z_CGATE, CONV_WIDTH)), w_brc_ref)
    yb = proj(yb_buf[...], w_bra_ref)

    merged = _sigmoid(zcol(Z_GCONV, D_MODEL)) * ya + _sigmoid(zcol(Z_GATTN, D_MODEL)) * yb
    x1 = x + _rmsnorm(proj(merged, w_out_ref), ln_post_ref[...])
    gate = _sigmoid(proj(x1, w_pg_ref))
    out_ref[...] = x1 + gate * proj(p_ref[...], w_pp_ref)


def _rope_lane_table():
    half = ROPE_DIM // 2
    inv = jnp.power(ROPE_THETA, -jnp.arange(0, ROPE_DIM, 2, dtype=jnp.float32) / ROPE_DIM)
    dim = jnp.arange(V7X_LANES) % HEAD_DIM
    inv_lane = jnp.where(dim < ROPE_DIM, inv[dim % half], 0.0)
    first = (dim < half).astype(jnp.float32)
    second = jnp.logical_and(dim >= half, dim < ROPE_DIM).astype(jnp.float32)
    pad = jnp.zeros((V7X_SUBLANES - 3, V7X_LANES), jnp.float32)
    return jnp.concatenate([inv_lane[None], first[None], second[None], pad], axis=0)


def _resident(shape):
    return pl.BlockSpec(shape, lambda b, s: (0,) * len(shape), pipeline_mode=pl.Buffered(1))


def _layer(x, p, pos, w_in, ln_pre, ln_post, w_dw, b_dw, cln_g, cln_b, w_pw, sinks,
           w_brc, w_bra, w_out, w_pg, w_pp):
    batch, seq, _ = x.shape
    t = TOKENS_PER_STEP
    assert seq % t == 0
    bf16 = jnp.bfloat16
    row = lambda a: a.reshape(1, -1).astype(jnp.float32)
    tile = lambda width: pl.BlockSpec((None, t, width), lambda b, s: (b, s, 0))
    w_cat = jnp.concatenate([w_in, w_pw], axis=1).astype(bf16)
    weights = [w.astype(bf16) for w in (w_brc, w_bra, w_out, w_pg, w_pp)]
    operands = [sinks.astype(jnp.float32), x, p, pos.astype(jnp.float32)[..., None], _rope_lane_table(),
                w_cat, row(ln_pre), row(ln_post), w_dw.astype(jnp.float32), row(b_dw),
                row(cln_g), row(cln_b)] + weights
    in_specs = [pl.BlockSpec(memory_space=pltpu.SMEM), tile(D_MODEL), tile(PLE_DIM), tile(1)]
    in_specs += [_resident(a.shape) for a in operands[4:]]
    scratch = [
        pltpu.VMEM((2, t, D_MODEL), bf16),
        pltpu.VMEM((CONV_HALO + t, CONV_WIDTH), jnp.float32),
        pltpu.VMEM((t, CONV_WIDTH), jnp.float32),
        pltpu.VMEM((t, Z_WIDTH), jnp.float32),
        pltpu.VMEM((t, ATTN_WIDTH), bf16),
        pltpu.VMEM((N_KV_HEADS, 2, BLOCK + t, V7X_LANES), bf16),
        pltpu.VMEM((N_KV_HEADS, 2, BLOCK + t, 2 * V7X_LANES), bf16),
        pltpu.VMEM((2, BLOCK, 2 * BLOCK), jnp.float32),
        pltpu.VMEM((t, ATTN_WIDTH), bf16),
    ]
    return pl.pallas_call(
        _layer_kernel,
        grid=(batch, seq // t),
        in_specs=in_specs,
        out_specs=tile(D_MODEL),
        out_shape=jax.ShapeDtypeStruct(x.shape, x.dtype),
        scratch_shapes=scratch,
        compiler_params=pltpu.CompilerParams(
            dimension_semantics=("arbitrary", "arbitrary"),
            vmem_limit_bytes=V7X_VMEM_BYTES - 8 * 1024 * 1024),
        name="hybrid_layer",
    )(*operands)


def kernel(x, p, positions, w_in, ln_pre, ln_post, w_dw, b_dw, conv_ln_g, conv_ln_b, w_pw, sinks,
           w_br_conv, w_br_attn, w_out, w_ple_gate, w_ple_proj):
    for i in range(w_in.shape[0]):
        x = _layer(x, p[i], positions, w_in[i], ln_pre[i], ln_post[i], w_dw[i], b_dw[i], conv_ln_g[i],
                   conv_ln_b[i], w_pw[i], sinks[i], w_br_conv[i], w_br_attn[i], w_out[i], w_ple_gate[i],
                   w_ple_proj[i])
    return x
```

```python
import math

import jax
import jax.numpy as jnp
from jax import lax
from jax.experimental import pallas as pl
from jax.experimental.pallas import tpu as pltpu

D_MODEL = 1024
PLE_DIM = 256
N_HEADS = 16
N_KV_HEADS = 2
HEAD_DIM = 64
GROUP = N_HEADS // N_KV_HEADS
ATTN_WIDTH = N_HEADS * HEAD_DIM
KV_WIDTH = N_KV_HEADS * HEAD_DIM
CONV_WIDTH = D_MODEL
CONV_KERNEL = 31
BLOCK = 128
ROPE_DIM = HEAD_DIM // 4
ROPE_THETA = 500000.0
EPS = 1e-6

COL_VAL = 0
COL_GLU = COL_VAL + CONV_WIDTH
COL_CGATE = COL_GLU + CONV_WIDTH
COL_Q = COL_CGATE + CONV_WIDTH
COL_K = COL_Q + ATTN_WIDTH
COL_V = COL_K + KV_WIDTH
COL_AGATE = COL_V + KV_WIDTH
COL_GCONV = COL_AGATE + ATTN_WIDTH
COL_GATTN = COL_GCONV + D_MODEL
IN_WIDTH = COL_GATTN + D_MODEL
COL_PW = IN_WIDTH
COL_BRC = COL_PW + CONV_WIDTH
COL_BRA = COL_BRC + D_MODEL
COL_OUT = COL_BRA + D_MODEL
COL_PG = COL_OUT + D_MODEL

V7X_LANES = 128
V7X_SUBLANES = 8
V7X_MXU_WIDTH = 256
V7X_VMEM_BYTES = 64 * 1024 * 1024

TOKENS_PER_STEP = 256
CONV_HALO = 32
CONV_ROWS = 64
LIVE_STAGE_RESULTS = 8
PAIRS = N_HEADS // 2
PAIRS_PER_GROUP = GROUP // 2
MASKED = -1e30

SLAB = V7X_MXU_WIDTH
Z_CGATE = 0
Z_Q = COL_Q - COL_CGATE
Z_KV = COL_K - COL_CGATE
Z_AGATE = COL_AGATE - COL_CGATE
Z_GCONV = COL_GCONV - COL_CGATE
Z_GATTN = COL_GATTN - COL_CGATE
Z_PW = COL_PW - COL_CGATE
Z_WIDTH = Z_PW + CONV_WIDTH
IN_SLABS = Z_PW // SLAB
ALL_SLABS = Z_WIDTH // SLAB

assert CONV_HALO >= CONV_KERNEL - 1 and CONV_HALO % V7X_SUBLANES == 0
assert TOKENS_PER_STEP % BLOCK == 0 and (TOKENS_PER_STEP // 2) % CONV_ROWS == 0
assert 2 * HEAD_DIM == V7X_LANES and KV_WIDTH == V7X_LANES
assert Z_WIDTH % SLAB == 0 and CONV_WIDTH % SLAB == 0 and CONV_WIDTH == ATTN_WIDTH


LOG2_E = 1.4426950408889634


def _sigmoid(t):
    return 1.0 / (1.0 + jnp.exp2(t * -LOG2_E))


def _silu(t):
    return t * _sigmoid(t)


def _rmsnorm(t, gain):
    return t * lax.rsqrt(jnp.mean(t * t, axis=-1, keepdims=True) + EPS) * gain


def _layer_kernel(sinks_ref, x_ref, p_ref, pos_ref, rope_ref, bias_ref, w_cat_ref, ln_pre_ref, ln_post_ref,
                  w_dw_ref, b_dw_ref, cln_g_ref, cln_b_ref, w_pp_ref, out_ref,
                  lhs_buf, u_buf, c_buf, z_buf, q_buf, kb_buf, vb_buf, yb_buf):
    tokens = x_ref.shape[0]
    seq_step = pl.program_id(1)
    bf16 = jnp.bfloat16
    f32 = jnp.float32
    n_conv_units = 2 * (CONV_WIDTH // V7X_LANES)
    n_attn_units = (tokens // BLOCK) * PAIRS
    assert n_conv_units <= IN_SLABS and ALL_SLABS - n_conv_units <= n_attn_units

    @pl.when(seq_step == 0)
    def _reset_carry():
        u_buf[0:CONV_HALO, :] = jnp.zeros((CONV_HALO, CONV_WIDTH), f32)
        kb_buf[:, :, 0:BLOCK, :] = jnp.zeros((N_KV_HEADS, 2, BLOCK, V7X_LANES), bf16)
        vb_buf[:, :, 0:BLOCK, :] = jnp.zeros((N_KV_HEADS, 2, BLOCK, 2 * V7X_LANES), bf16)

    @pl.when(seq_step > 0)
    def _shift_carry():
        u_buf[0:CONV_HALO, :] = u_buf[tokens:tokens + CONV_HALO, :]
        for buf in (kb_buf, vb_buf):
            buf[:, :, 0:BLOCK, :] = buf[:, :, tokens:tokens + BLOCK, :]

    def slab_projection(slab, lhs_index):
        col = slab * SLAB
        z_buf[:, pl.ds(col, SLAB)] = jnp.dot(lhs_buf[lhs_index], w_cat_ref[:, pl.ds(COL_CGATE + col, SLAB)],
                                             preferred_element_type=f32)

    x = x_ref[...]
    h = _rmsnorm(x, ln_pre_ref[...]).astype(bf16)
    lhs_buf[0] = h

    ang = rope_ref[...] * pos_ref[...]
    cos_f = jnp.cos(ang)
    sin_f = jnp.sin(ang)
    one = jnp.ones_like(cos_f)
    zero = jnp.zeros_like(cos_f)

    def lane_table(first_half, second_half, rest):
        head = [first_half, second_half] + [rest] * (HEAD_DIM // (ROPE_DIM // 2) - 2)
        return jnp.concatenate(head * (V7X_LANES // HEAD_DIM), axis=0).T

    cos_t = lane_table(cos_f, cos_f, one)
    sin_t = lane_table(-sin_f, sin_f, zero)
    lane_in_head = lax.broadcasted_iota(jnp.int32, (tokens, V7X_LANES), 1) % HEAD_DIM
    first_half = lane_in_head < ROPE_DIM // 2
    q_scale = LOG2_E * HEAD_DIM ** -0.5
    cos_q, sin_q = cos_t * q_scale, sin_t * q_scale

    def rope(t, cos_table, sin_table):
        up = pltpu.roll(t, V7X_LANES - ROPE_DIM // 2, 1)
        down = pltpu.roll(t, ROPE_DIM // 2, 1)
        return t * cos_table + jnp.where(first_half, up, down) * sin_table

    val = jnp.dot(h, w_cat_ref[:, COL_VAL:COL_GLU], preferred_element_type=f32)
    glu = jnp.dot(h, w_cat_ref[:, COL_GLU:COL_CGATE], preferred_element_type=f32)
    u_buf[CONV_HALO:CONV_HALO + tokens, :] = val * _sigmoid(glu)
    first_tap_row = CONV_HALO - (CONV_KERNEL - 1)

    def conv_chunk(r0, l0, heads):
        lanes = pl.ds(l0, V7X_LANES)
        window = u_buf[pl.ds(r0, CONV_ROWS + CONV_HALO), lanes]
        out = None
        for shift in range(V7X_SUBLANES):
            rows = CONV_ROWS if shift == 0 else CONV_ROWS + V7X_SUBLANES
            known = V7X_SUBLANES if shift in heads else 0
            part = None
            for a in range(CONV_HALO // V7X_SUBLANES + 1):
                tap = V7X_SUBLANES * a + shift - first_tap_row
                if not 0 <= tap < CONV_KERNEL:
                    continue
                first = V7X_SUBLANES * a
                term = w_dw_ref[tap:tap + 1, lanes] * window[first + known:first + rows, :]
                part = term if part is None else part + term
            if known:
                part = jnp.concatenate([heads[shift], part], axis=0)
            if shift:
                heads[shift] = part[CONV_ROWS:, :]
                part = part[shift:shift + CONV_ROWS, :]
            out = part if out is None else out + part
        return out

    half_rows = tokens // 2
    for unit in range(n_conv_units):
        l0 = (unit // 2) * V7X_LANES
        heads = {}
        for r0 in range((unit % 2) * half_rows, (unit % 2 + 1) * half_rows, CONV_ROWS):
            c_buf[pl.ds(r0, CONV_ROWS), pl.ds(l0, V7X_LANES)] = conv_chunk(r0, l0, heads)
        slab_projection(unit, 0)

    c = c_buf[...] + b_dw_ref[...]
    mu = jnp.mean(c, axis=-1, keepdims=True)
    cc = c - mu
    var = jnp.mean(cc * cc, axis=-1, keepdims=True)
    c = cc * lax.rsqrt(var + EPS) * cln_g_ref[...] + cln_b_ref[...]
    lhs_buf[1] = _silu(c).astype(bf16)

    for l0 in range(0, ATTN_WIDTH, V7X_LANES):
        q_buf[:, l0:l0 + V7X_LANES] = rope(z_buf[:, Z_Q + l0:Z_Q + l0 + V7X_LANES], cos_q, sin_q).astype(bf16)
    k = rope(z_buf[:, Z_KV:Z_KV + KV_WIDTH], cos_t, sin_t)
    v = z_buf[:, Z_KV + KV_WIDTH:Z_AGATE]
    low_half = lax.broadcasted_iota(jnp.int32, (tokens, V7X_LANES), 1) < HEAD_DIM
    zero = jnp.zeros((tokens, V7X_LANES), f32)
    new_rows = slice(BLOCK, BLOCK + tokens)
    for src, dst in ((k, kb_buf), (v, vb_buf)):
        swapped = pltpu.roll(src, HEAD_DIM, 1)
        dst[0, 0, new_rows, 0:V7X_LANES] = jnp.where(low_half, src, zero).astype(bf16)
        dst[0, 1, new_rows, 0:V7X_LANES] = jnp.where(low_half, zero, swapped).astype(bf16)
        dst[1, 0, new_rows, 0:V7X_LANES] = jnp.where(low_half, swapped, zero).astype(bf16)
        dst[1, 1, new_rows, 0:V7X_LANES] = jnp.where(low_half, zero, src).astype(bf16)
    ones_lo = jnp.where(low_half, 1.0, 0.0).astype(bf16)
    ones_hi = jnp.where(low_half, 0.0, 1.0).astype(bf16)
    for g in range(N_KV_HEADS):
        vb_buf[g, 0, new_rows, V7X_LANES:] = ones_lo
        vb_buf[g, 1, new_rows, V7X_LANES:] = ones_hi

    low_half_blk = lax.broadcasted_iota(jnp.int32, (BLOCK, V7X_LANES), 1) < HEAD_DIM

    def attn_unit(unit):
        blk, pair = divmod(unit, PAIRS)
        g = pair // PAIRS_PER_GROUP
        r0 = blk * BLOCK
        l0 = pair * V7X_LANES
        rows = pl.ds(r0, BLOCK)
        lanes = pl.ds(l0, V7X_LANES)
        window = pl.ds(r0, 2 * BLOCK)
        k_blockdiag = jnp.concatenate([kb_buf[g, 0, window, :], kb_buf[g, 1, window, :]], axis=0)
        v_blockdiag = jnp.concatenate([vb_buf[g, 0, window, :], vb_buf[g, 1, window, :]], axis=0)
        scores = lax.dot_general(q_buf[rows, lanes], k_blockdiag,
                                 (((1,), (1,)), ((), ())), preferred_element_type=f32)
        bias = jnp.where(seq_step == 0, bias_ref[1], bias_ref[0]) if blk == 0 else bias_ref[0]
        exps, sink_terms = [], []
        for half in range(2):
            sink = sinks_ref[2 * pair + half] * LOG2_E
            s_h = scores[:, half * 2 * BLOCK:(half + 1) * 2 * BLOCK] + bias
            m = jnp.maximum(jnp.max(s_h, axis=-1, keepdims=True), sink)
            exps.append(jnp.exp2(s_h - m).astype(bf16))
            sink_terms.append(jnp.exp2(sink - m))
        o = jnp.dot(jnp.concatenate(exps, axis=1), v_blockdiag, preferred_element_type=f32)
        den = o[:, V7X_LANES:] + jnp.where(low_half_blk, sink_terms[0], sink_terms[1])
        a_gate = z_buf[rows, pl.ds(Z_AGATE + l0, V7X_LANES)]
        yb_buf[rows, lanes] = (o[:, :V7X_LANES] * (1.0 / den) * _silu(a_gate)).astype(bf16)

    assert ALL_SLABS - n_conv_units <= PAIRS + 1
    for unit in range(PAIRS):
        attn_unit(unit)
        slab = n_conv_units + unit
        slab_projection(slab, 0 if slab < IN_SLABS else 1)
    for slab in range(n_conv_units + PAIRS, ALL_SLABS):
        slab_projection(slab, 0 if slab < IN_SLABS else 1)

    for unit in range(PAIRS, n_attn_units):
        attn_unit(unit)

    def proj(act, col):
        return jnp.dot(act.astype(bf16), w_cat_ref[:, col:col + D_MODEL], preferred_element_type=f32)

    zcol = lambda lo, width: z_buf[:, lo:lo + width]
    ya = proj(zcol(Z_PW, CONV_WIDTH) * _silu(zcol(Z_CGATE, CONV_WIDTH)), COL_BRC)
    yb = proj(yb_buf[...], COL_BRA)

    merged = _sigmoid(zcol(Z_GCONV, D_MODEL)) * ya + _sigmoid(zcol(Z_GATTN, D_MODEL)) * yb
    x1 = x + _rmsnorm(proj(merged, COL_OUT), ln_post_ref[...])
    gate = _sigmoid(proj(x1, COL_PG))
    ple = jnp.dot(p_ref[...].astype(bf16), w_pp_ref[...], preferred_element_type=f32)
    out_ref[...] = x1 + gate * ple


def _rope_frequency_table(tokens):
    inv = jnp.power(ROPE_THETA, -jnp.arange(0, ROPE_DIM, 2, dtype=jnp.float32) / ROPE_DIM)
    return jnp.broadcast_to(inv[:, None], (ROPE_DIM // 2, tokens))


def _band_bias_table():
    q_pos = lax.broadcasted_iota(jnp.int32, (BLOCK, 2 * BLOCK), 0)
    k_pos = lax.broadcasted_iota(jnp.int32, (BLOCK, 2 * BLOCK), 1)
    in_band = jnp.logical_and(k_pos <= q_pos + BLOCK, k_pos > q_pos)
    first = jnp.logical_and(in_band, k_pos >= BLOCK)
    return jnp.where(jnp.stack([in_band, first]), 0.0, MASKED).astype(jnp.float32)


def _resident(shape):
    return pl.BlockSpec(shape, lambda b, s: (0,) * len(shape), pipeline_mode=pl.Buffered(1))


def _layer(x, p, pos, w_in, ln_pre, ln_post, w_dw, b_dw, cln_g, cln_b, w_pw, sinks,
           w_brc, w_bra, w_out, w_pg, w_pp):
    batch, seq, _ = x.shape
    t = TOKENS_PER_STEP
    assert seq % t == 0
    bf16 = jnp.bfloat16
    row = lambda a: a.reshape(1, -1).astype(jnp.float32)
    tile = lambda width: pl.BlockSpec((None, t, width), lambda b, s: (b, s, 0))
    w_cat = jnp.concatenate([w_in, w_pw, w_brc, w_bra, w_out, w_pg], axis=1).astype(bf16)
    assert w_cat.shape[1] == COL_PG + D_MODEL
    weights = [w_pp.astype(bf16)]
    operands = [sinks.astype(jnp.float32), x, p, pos.astype(jnp.float32)[:, None, :], _rope_frequency_table(t),
                _band_bias_table(),
                w_cat, row(ln_pre), row(ln_post), w_dw.astype(jnp.float32), row(b_dw),
                row(cln_g), row(cln_b)] + weights
    pos_tile = pl.BlockSpec((None, 1, t), lambda b, s: (b, 0, s))
    in_specs = [pl.BlockSpec(memory_space=pltpu.SMEM), tile(D_MODEL), tile(PLE_DIM), pos_tile]
    in_specs += [_resident(a.shape) for a in operands[4:]]
    scratch = [
        pltpu.VMEM((2, t, D_MODEL), bf16),
        pltpu.VMEM((CONV_HALO + t, CONV_WIDTH), jnp.float32),
        pltpu.VMEM((t, CONV_WIDTH), jnp.float32),
        pltpu.VMEM((t, Z_WIDTH), jnp.float32),
        pltpu.VMEM((t, ATTN_WIDTH), bf16),
        pltpu.VMEM((N_KV_HEADS, 2, BLOCK + t, V7X_LANES), bf16),
        pltpu.VMEM((N_KV_HEADS, 2, BLOCK + t, 2 * V7X_LANES), bf16),
        pltpu.VMEM((t, ATTN_WIDTH), bf16),
    ]
    tile_bytes = t * (2 * D_MODEL + PLE_DIM + V7X_SUBLANES) * 4
    vmem_bytes = (sum(a.size * a.dtype.itemsize for a in operands[4:]) + 2 * tile_bytes
                  + sum(math.prod(s.shape) * jnp.dtype(s.dtype).itemsize for s in scratch)
                  + LIVE_STAGE_RESULTS * t * D_MODEL * 4)
    assert vmem_bytes <= V7X_VMEM_BYTES, vmem_bytes
    return pl.pallas_call(
        _layer_kernel,
        grid=(batch, seq // t),
        in_specs=in_specs,
        out_specs=tile(D_MODEL),
        out_shape=jax.ShapeDtypeStruct(x.shape, x.dtype),
        scratch_shapes=scratch,
        compiler_params=pltpu.CompilerParams(
            dimension_semantics=("arbitrary", "arbitrary"),
            vmem_limit_bytes=vmem_bytes),
        name="hybrid_layer",
    )(*operands)


def kernel(x, p, positions, w_in, ln_pre, ln_post, w_dw, b_dw, conv_ln_g, conv_ln_b, w_pw, sinks,
           w_br_conv, w_br_attn, w_out, w_ple_gate, w_ple_proj):
    for i in range(w_in.shape[0]):
        x = _layer(x, p[i], positions, w_in[i], ln_pre[i], ln_post[i], w_dw[i], b_dw[i], conv_ln_g[i],
                   conv_ln_b[i], w_pw[i], sinks[i], w_br_conv[i], w_br_attn[i], w_out[i], w_ple_gate[i],
                   w_ple_proj[i])
    return x
```
